```python
import math
import jax, jax.numpy as jnp
from jax import lax
import numpy as np

D_MODEL = 4096
BATCH = 1
SEQ = 8192
DEPTH = 2

CHUNK = 128
RMS_EPS = 1e-6
E_A = D_MODEL
P_A = 64
H_A = E_A // P_A
G_A = 8
N_A = 128
K_A = 4
CONV_A = E_A + 2 * G_A * N_A
E_B = D_MODEL // 2
POOL_WINDOWS = (2, 4, 8, 16)
N_POOL = len(POOL_WINDOWS)
C_POOL = E_B // N_POOL
DK_C = 128
DV_C = 128
H_C = (D_MODEL // 2) // DV_C
E_C = H_C * DV_C
ROPE_BASE = 10000.0
N_BRANCH = 3
D_FF = 256 * ((int(2 * 4 * D_MODEL / 3) + 255) // 256)
K_F = 3
IN_SIZES = (E_A, CONV_A, H_A, E_B, H_C * DK_C, H_C * DK_C, E_C, E_C, N_BRANCH * D_MODEL)
N_IN = sum(IN_SIZES)

kernel_name = "hybrid_ssd_pool_retention_parallel_gated"


def rms_norm(x, g, eps=RMS_EPS):
    xf = x.astype(jnp.float32)
    y = xf * lax.rsqrt(jnp.mean(xf * xf, axis=-1, keepdims=True) + eps)
    return (y * g.astype(jnp.float32)).astype(x.dtype)


def causal_dwconv(u, w, b):
    K = w.shape[0]
    L = u.shape[1]
    up = jnp.pad(u, ((0, 0), (K - 1, 0), (0, 0)))
    y = b
    for k in range(K):
        y = y + up[:, k:k + L, :] * w[k]
    return y


def rotary(x, pos):
    half = x.shape[-1] // 2
    inv = ROPE_BASE ** (-jnp.arange(half, dtype=jnp.float32) / half)
    ang = pos.astype(jnp.float32)[..., None] * inv
    cos = jnp.cos(ang)[:, :, None, :]
    sin = jnp.sin(ang)[:, :, None, :]
    x1, x2 = x[..., :half], x[..., half:]
    return jnp.concatenate([x1 * cos - x2 * sin, x1 * sin + x2 * cos], axis=-1)


def ssd_chunked(xs, Bm, Cm, dt, a_log, d_skip):
    b, L, H, P = xs.shape
    G, N = Bm.shape[2], Bm.shape[3]
    R = H // G
    c = L // CHUNK
    xf = xs.astype(jnp.float32).reshape(b, c, CHUNK, G, R, P)
    Bf = Bm.astype(jnp.float32).reshape(b, c, CHUNK, G, N)
    Cf = Cm.astype(jnp.float32).reshape(b, c, CHUNK, G, N)
    dtc = dt.reshape(b, c, CHUNK, G, R)
    A = -jnp.exp(a_log.astype(jnp.float32)).reshape(G, R)
    cs = jnp.cumsum(dtc * A, axis=2)
    xdt = xf * dtc[..., None]
    causal = jnp.tril(jnp.ones((CHUNK, CHUNK), dtype=bool))
    seg = cs[:, :, :, None] - cs[:, :, None, :]
    decay_ls = jnp.exp(jnp.where(causal[:, :, None, None], seg, -jnp.inf))
    cb = jnp.einsum('bclgn,bcsgn->bclsg', Cf, Bf)
    y_diag = jnp.einsum('bclsgr,bcsgrp->bclgrp', cb[..., None] * decay_ls, xdt)
    decay_to_end = jnp.exp(cs[:, :, -1:] - cs)
    chunk_states = jnp.einsum('bclgn,bclgrp->bcgrpn', Bf, xdt * decay_to_end[..., None])
    chunk_decay = jnp.exp(cs[:, :, -1])

    def step(state, inp):
        dec, st = inp
        return state * dec[..., None, None] + st, state

    _, prev = lax.scan(step, jnp.zeros((b, G, R, P, N), jnp.float32),
                       (jnp.moveaxis(chunk_decay, 1, 0), jnp.moveaxis(chunk_states, 1, 0)))
    prev = jnp.moveaxis(prev, 0, 1)
    y_off = jnp.einsum('bclgn,bcgrpn->bclgrp', Cf, prev) * jnp.exp(cs)[..., None]
    y = y_diag + y_off + xf * d_skip.astype(jnp.float32).reshape(G, R)[:, :, None]
    return y.reshape(b, L, H * P)


def retention_chunkwise(q, k, v):
    b, L, H, Dk = q.shape
    Dv = v.shape[-1]
    c = L // CHUNK
    lg = jnp.log1p(-jnp.exp2(-5.0 - jnp.arange(H, dtype=jnp.float32)))
    idx = jnp.arange(CHUNK, dtype=jnp.float32)
    causal = jnp.tril(jnp.ones((CHUNK, CHUNK), dtype=bool))
    rel = jnp.where(causal, idx[:, None] - idx[None, :], 0.0)
    Dmat = jnp.where(causal[None], jnp.exp(rel[None] * lg[:, None, None]), 0.0)
    qc = q.reshape(b, c, CHUNK, H, Dk)
    kc = k.reshape(b, c, CHUNK, H, Dk)
    vc = v.reshape(b, c, CHUNK, H, Dv)
    scores = jnp.einsum('bclhd,bcshd->bchls', qc, kc) * Dmat
    y_in = jnp.einsum('bchls,bcshe->bclhe', scores, vc)
    k_dec = kc * jnp.exp((CHUNK - 1 - idx)[:, None] * lg)[:, :, None]
    chunk_kv = jnp.einsum('bcshd,bcshe->bchde', k_dec, vc)
    chunk_decay = jnp.exp(CHUNK * lg)

    def step(state, kv):
        return state * chunk_decay[:, None, None] + kv, state

    _, prev = lax.scan(step, jnp.zeros((b, H, Dk, Dv), jnp.float32), jnp.moveaxis(chunk_kv, 1, 0))
    prev = jnp.moveaxis(prev, 0, 1)
    q_dec = qc * jnp.exp((idx + 1.0)[:, None] * lg)[:, :, None]
    y_cross = jnp.einsum('bclhd,bchde->bclhe', q_dec, prev)
    return (y_in + y_cross).reshape(b, L, H, Dv)


def multiscale_pool(u, pool_w, pool_scale):
    b, L, C = u.shape
    uf = u.astype(jnp.float32)
    cs0 = jnp.pad(jnp.cumsum(uf, axis=1), ((0, 0), (1, 0), (0, 0)))
    t = jnp.arange(L)
    outs = []
    for gi, w in enumerate(POOL_WINDOWS):
        c_g = cs0[:, :, gi * C_POOL:(gi + 1) * C_POOL]
        lag = jnp.pad(c_g, ((0, 0), (w - 1, 0), (0, 0)))[:, :L]
        cnt = jnp.minimum(t + 1, w).astype(jnp.float32)
        mean = (c_g[:, 1:] - lag) / cnt[None, :, None]
        outs.append(mean - uf[:, :, gi * C_POOL:(gi + 1) * C_POOL])
    pooled = jnp.stack(outs, axis=2).astype(u.dtype)
    mixed = jnp.einsum('blgc,gcd->blgd', pooled, pool_w)
    return mixed.reshape(b, L, C) * pool_scale


def setup_inputs(seed: int = 0) -> dict:
    key = jax.random.key(seed)
    ks = jax.random.split(key, 32)

    def nrm(k, shape, scale):
        return jax.random.normal(k, shape, jnp.float32) * scale

    def gain(k, shape):
        return 1.0 + 0.02 * jax.random.normal(k, shape, jnp.float32)

    x = jax.random.normal(ks[0], (BATCH, SEQ, D_MODEL), jnp.float32)
    start = jax.random.randint(ks[1], (BATCH, 1), 0, 4096, dtype=jnp.int32)
    positions = start + jnp.arange(SEQ, dtype=jnp.int32)[None, :]
    dt0 = jnp.exp(jax.random.uniform(ks[8], (DEPTH, H_A), jnp.float32)
                  * (math.log(0.1) - math.log(1e-3)) + math.log(1e-3))
    dt_bias = dt0 + jnp.log(-jnp.expm1(-dt0))
    a_log = jnp.log(jax.random.uniform(ks[9], (DEPTH, H_A), jnp.float32, 1.0, 16.0))
    return {
        "x": x,
        "positions": positions,
        "norm_mix": gain(ks[2], (DEPTH, D_MODEL)),
        "w_in": nrm(ks[3], (DEPTH, D_MODEL, N_IN), D_MODEL ** -0.5),
        "b_gate": nrm(ks[4], (DEPTH, N_BRANCH * D_MODEL), 0.1),
        "conv_a_w": nrm(ks[5], (DEPTH, K_A, CONV_A), K_A ** -0.5),
        "conv_a_b": nrm(ks[6], (DEPTH, CONV_A), 0.01),
        "dt_bias": dt_bias,
        "a_log": a_log,
        "d_skip": gain(ks[10], (DEPTH, H_A)),
        "norm_a": gain(ks[11], (DEPTH, E_A)),
        "pool_w": nrm(ks[12], (DEPTH, N_POOL, C_POOL, C_POOL), C_POOL ** -0.5),
        "pool_scale": gain(ks[13], (DEPTH, E_B)),
        "norm_c": gain(ks[14], (DEPTH, DV_C)),
        "w_br_a": nrm(ks[15], (DEPTH, E_A, D_MODEL), E_A ** -0.5),
        "w_br_b": nrm(ks[16], (DEPTH, E_B, D_MODEL), E_B ** -0.5),
        "w_br_c": nrm(ks[17], (DEPTH, E_C, D_MODEL), E_C ** -0.5),
        "w_out": nrm(ks[18], (DEPTH, D_MODEL, D_MODEL), D_MODEL ** -0.5),
        "norm_ffn": gain(ks[19], (DEPTH, D_MODEL)),
        "w_up": nrm(ks[20], (DEPTH, D_MODEL, 2 * D_FF), D_MODEL ** -0.5),
        "conv_f_w": nrm(ks[21], (DEPTH, K_F, 2 * D_FF), K_F ** -0.5),
        "conv_f_b": nrm(ks[22], (DEPTH, 2 * D_FF), 0.01),
        "w_down": nrm(ks[23], (DEPTH, D_FF, D_MODEL), D_FF ** -0.5),
        "norm_f": gain(ks[24], (D_MODEL,)),
    }


def reference(x, positions, norm_mix, w_in, b_gate, conv_a_w, conv_a_b, dt_bias, a_log,
              d_skip, norm_a, pool_w, pool_scale, norm_c, w_br_a, w_br_b, w_br_c, w_out,
              norm_ffn, w_up, conv_f_w, conv_f_b, w_down, norm_f):
    b, L, _ = x.shape
    offs = np.cumsum((0,) + IN_SIZES).tolist()
    for i in range(DEPTH):
        h = rms_norm(x, norm_mix[i])
        proj = h @ w_in[i]
        z_a, xbc_a, dt_a, u_b, q_c, k_c, v_c, g_c, gate_raw = [
            proj[..., offs[j]:offs[j + 1]] for j in range(len(IN_SIZES))]

        xbc = jax.nn.silu(causal_dwconv(xbc_a, conv_a_w[i], conv_a_b[i]))
        xs = xbc[..., :E_A].reshape(b, L, H_A, P_A)
        Bm = xbc[..., E_A:E_A + G_A * N_A].reshape(b, L, G_A, N_A)
        Cm = xbc[..., E_A + G_A * N_A:].reshape(b, L, G_A, N_A)
        dt = jax.nn.softplus((dt_a + dt_bias[i]).astype(jnp.float32))
        y_ssd = ssd_chunked(xs, Bm, Cm, dt, a_log[i], d_skip[i])
        yz = (y_ssd * jax.nn.silu(z_a.astype(jnp.float32))).reshape(b, L, G_A, E_A // G_A)
        yz = yz * lax.rsqrt(jnp.mean(yz * yz, axis=-1, keepdims=True) + RMS_EPS)
        y_a = (yz.reshape(b, L, E_A) * norm_a[i].astype(jnp.float32)).astype(x.dtype)

        y_b = multiscale_pool(u_b, pool_w[i], pool_scale[i]).astype(x.dtype)

        q = rotary(q_c.astype(jnp.float32).reshape(b, L, H_C, DK_C), positions) * (DK_C ** -0.5)
        k = rotary(k_c.astype(jnp.float32).reshape(b, L, H_C, DK_C), positions)
        v = v_c.astype(jnp.float32).reshape(b, L, H_C, DV_C)
        ret = retention_chunkwise(q, k, v)
        ret = ret * lax.rsqrt(jnp.mean(ret * ret, axis=-1, keepdims=True) + RMS_EPS)
        ret = ret * norm_c[i].astype(jnp.float32)
        y_c = (ret.reshape(b, L, E_C) * jax.nn.silu(g_c.astype(jnp.float32))).astype(x.dtype)

        gates = jax.nn.sigmoid((gate_raw + b_gate[i]).astype(jnp.float32)).reshape(b, L, N_BRANCH, D_MODEL)
        merged = (gates[:, :, 0] * (y_a @ w_br_a[i]).astype(jnp.float32)
                  + gates[:, :, 1] * (y_b @ w_br_b[i]).astype(jnp.float32)
                  + gates[:, :, 2] * (y_c @ w_br_c[i]).astype(jnp.float32)).astype(x.dtype)
        x = x + merged @ w_out[i]

        h = rms_norm(x, norm_ffn[i])
        up = causal_dwconv(h @ w_up[i], conv_f_w[i], conv_f_b[i])
        act = jax.nn.silu(up[..., :D_FF]) * up[..., D_FF:]
        x = x + act @ w_down[i]
    return rms_norm(x, norm_f)
```

```python
import functools

import jax
import jax.numpy as jnp
from jax import lax
from jax.experimental import pallas as pl
from jax.experimental.pallas import tpu as pltpu

CHUNK = 128
RMS_EPS = 1e-6
P_A = 64
G_A = 8
N_A = 128
K_A = 4
POOL_WINDOWS = (2, 4, 8, 16)
DK_C = 128
DV_C = 128
ROPE_BASE = 10000.0
K_F = 3
N_BRANCH = 3

LANES = 128
SUBLANES = 8
V7X_VMEM_LIMIT = 56 * 1024 * 1024
HALO = 16

F32 = jnp.float32
BF16 = jnp.bfloat16


def _tile(n, target, *offsets):
    if n <= LANES:
        return n
    best = None
    t = LANES
    while t <= min(n, target):
        if n % t == 0 and all(o % t == 0 for o in offsets):
            best = t
        t += LANES
    assert best is not None, (n, target, offsets)
    return best


def _params(n_axes, vmem_bytes):
    limit = int(min(V7X_VMEM_LIMIT, max(vmem_bytes, 16 * 1024 * 1024)))
    return pltpu.CompilerParams(
        dimension_semantics=("arbitrary",) * n_axes, vmem_limit_bytes=limit)


def _sigmoid(x):
    return 1.0 / (1.0 + jnp.exp(-x))


def _silu(x):
    return x * _sigmoid(x)


def _rmsnorm_kernel(x_ref, g_ref, o_ref):
    x = x_ref[...]
    ms = jnp.mean(x * x, axis=-1, keepdims=True)
    o_ref[...] = (x * lax.rsqrt(ms + RMS_EPS) * g_ref[...]).astype(o_ref.dtype)


def _rmsnorm(x, g, out_dtype):
    m, d = x.shape
    tm = min(m, 512)
    vmem = 2 * tm * d * 4 * 2 + 4 * tm * d * 4
    return pl.pallas_call(
        _rmsnorm_kernel,
        out_shape=jax.ShapeDtypeStruct((m, d), out_dtype),
        grid=(m // tm,),
        in_specs=[pl.BlockSpec((tm, d), lambda i: (i, 0)),
                  pl.BlockSpec((1, d), lambda i: (0, 0))],
        out_specs=pl.BlockSpec((tm, d), lambda i: (i, 0)),
        compiler_params=_params(1, vmem),
        name="rmsnorm",
    )(x, g.reshape(1, d))


def _mm_kernel(a_ref, w_ref, *rest, mode, nk, has_prev):
    if nk > 1:
        acc_ref = rest[-1]
        rest = rest[:-1]
    o_ref = rest[-1]
    extras = rest[:-1]
    acc = jnp.dot(a_ref[...], w_ref[...], preferred_element_type=F32)

    def finish(acc):
        if mode == "plain":
            out = acc
        elif mode == "resid":
            out = extras[0][...] + acc
        else:
            out = _sigmoid(extras[0][...] + extras[1][...]) * acc
            if has_prev:
                out = extras[2][...] + out
        o_ref[...] = out.astype(o_ref.dtype)

    if nk == 1:
        finish(acc)
    else:
        k = pl.program_id(2)

        @pl.when(k == 0)
        def _():
            acc_ref[...] = acc

        @pl.when(k > 0)
        def _():
            acc_ref[...] += acc

        @pl.when(k == nk - 1)
        def _():
            finish(acc_ref[...])


def _mm(a, w, *, mode="plain", out_dtype=F32, tm=1024, tn=512, tk=None,
        resid=None, gate_src=None, gate_off=0, gate_bias=None, prev=None, name="mm"):
    m, kdim = a.shape
    n = w.shape[1]
    tm = min(tm, m)
    tn = _tile(n, tn, gate_off)
    tk = kdim if tk is None else tk
    nk = kdim // tk
    assert m % tm == 0 and n % tn == 0 and kdim % tk == 0
    in_specs = [pl.BlockSpec((tm, tk), lambda j, i, k: (i, k)),
                pl.BlockSpec((tk, tn), lambda j, i, k: (k, j))]
    args = [a, w]
    tile = lambda j, i, k: (i, j)
    n_tiles = 1
    if mode == "resid":
        in_specs.append(pl.BlockSpec((tm, tn), tile))
        args.append(resid)
        n_tiles += 1
    elif mode == "gate":
        goff = gate_off // tn
        in_specs.append(pl.BlockSpec((tm, tn), lambda j, i, k: (i, goff + j)))
        in_specs.append(pl.BlockSpec((1, tn), lambda j, i, k: (0, j)))
        args += [gate_src, gate_bias.reshape(1, n)]
        n_tiles += 1
        if prev is not None:
            in_specs.append(pl.BlockSpec((tm, tn), tile))
            args.append(prev)
            n_tiles += 1
    scratch = [pltpu.VMEM((tm, tn), F32)] if nk > 1 else []
    vmem = (2 * tm * tk * a.dtype.itemsize + 2 * tk * tn * w.dtype.itemsize
            + 2 * n_tiles * tm * tn * 4 + 3 * tm * tn * 4)
    return pl.pallas_call(
        functools.partial(_mm_kernel, mode=mode, nk=nk, has_prev=prev is not None),
        out_shape=jax.ShapeDtypeStruct((m, n), out_dtype),
        grid=(n // tn, m // tm, nk),
        in_specs=in_specs,
        out_specs=pl.BlockSpec((tm, tn), tile),
        scratch_shapes=scratch,
        compiler_params=_params(3, vmem),
        name=name,
    )(*args)


def _dwconv_silu_kernel(x_ref, w_ref, b_ref, o_ref, ext_ref, *, ts, kw):
    s = pl.program_id(1)

    @pl.when(s == 0)
    def _():
        ext_ref[0:HALO, :] = jnp.zeros((HALO, ext_ref.shape[1]), F32)

    cur = x_ref[...]
    ext_ref[HALO:HALO + ts, :] = cur
    y = b_ref[...]
    for k in range(kw):
        off = HALO - (kw - 1) + k
        y = y + ext_ref[off:off + ts, :] * w_ref[k:k + 1, :]
    o_ref[...] = _silu(y).astype(o_ref.dtype)
    ext_ref[0:HALO, :] = cur[ts - HALO:ts, :]


def _dwconv_silu(src, col_off, width, w, b, out_dtype, name):
    l = src.shape[0]
    kw = w.shape[0]
    ts = min(l, 1024)
    tc = _tile(width, 512, col_off)
    coff = col_off // tc
    vmem = 6 * ts * tc * 4 + 4 * ts * tc * 4
    return pl.pallas_call(
        functools.partial(_dwconv_silu_kernel, ts=ts, kw=kw),
        out_shape=jax.ShapeDtypeStruct((l, width), out_dtype),
        grid=(width // tc, l // ts),
        in_specs=[pl.BlockSpec((ts, tc), lambda j, s: (s, coff + j)),
                  pl.BlockSpec((kw, tc), lambda j, s: (0, j)),
                  pl.BlockSpec((1, tc), lambda j, s: (0, j))],
        out_specs=pl.BlockSpec((ts, tc), lambda j, s: (s, j)),
        scratch_shapes=[pltpu.VMEM((HALO + ts, tc), F32)],
        compiler_params=_params(2, vmem),
        name=name,
    )(src, w, b.reshape(1, width))


def _dt_kernel(raw_ref, bias_ref, alog_ref, dt_ref, cs_ref):
    x = raw_ref[...] + bias_ref[...]
    dt = jnp.maximum(x, 0.0) + jnp.log1p(jnp.exp(-jnp.abs(x)))
    dt_ref[...] = dt
    cs = dt * (-jnp.exp(alog_ref[...]))
    row = lax.broadcasted_iota(jnp.int32, cs.shape, 0)
    sh = 1
    while sh < CHUNK:
        cs = cs + jnp.where(row >= sh, pltpu.roll(cs, sh, 0), 0.0)
        sh *= 2
    cs_ref[...] = cs


def _dt_prep(raw, bias, a_log):
    l, hp = raw.shape
    spec = pl.BlockSpec((CHUNK, hp), lambda c: (c, 0))
    vec = pl.BlockSpec((1, hp), lambda c: (0, 0))
    return pl.pallas_call(
        _dt_kernel,
        out_shape=(jax.ShapeDtypeStruct((l, hp), F32), jax.ShapeDtypeStruct((l, hp), F32)),
        grid=(l // CHUNK,),
        in_specs=[spec, vec, vec],
        out_specs=(spec, spec),
        compiler_params=_params(1, 0),
        name="ssd_dt",
    )(raw, bias, a_log)


def _ssd_kernel(xs_ref, b_ref, c_ref, z_ref, dtc_ref, csc_ref, csr_ref, dskip_ref, na_ref,
                o_ref, state_ref, *, npairs):
    c = pl.program_id(1)

    @pl.when(c == 0)
    def _():
        state_ref[...] = jnp.zeros(state_ref.shape, F32)

    xs = xs_ref[...]
    bm = b_ref[...].astype(BF16)
    cm = c_ref[...].astype(BF16)
    dtc = dtc_ref[...]
    csc = csc_ref[...]
    csr = csr_ref[...]

    lane = lax.broadcasted_iota(jnp.int32, (CHUNK, LANES), 1)
    lo = lane < P_A

    def expand(v):
        tiles = [jnp.where(lo, v[:, 2 * k:2 * k + 1], v[:, 2 * k + 1:2 * k + 2])
                 for k in range(npairs)]
        return tiles[0] if npairs == 1 else jnp.concatenate(tiles, axis=1)

    dt_e = expand(dtc)
    cs_e = expand(csc)
    xdt = xs * dt_e
    last = cs_e[CHUNK - 1:CHUNK, :]
    xdt_b = xdt.astype(BF16)

    cb = lax.dot_general(cm, bm, (((1,), (1,)), ((), ())), preferred_element_type=F32)
    row = lax.broadcasted_iota(jnp.int32, (CHUNK, CHUNK), 0)
    col = lax.broadcasted_iota(jnp.int32, (CHUNK, CHUNK), 1)
    causal = row >= col

    def decay_mat(r):
        seg = csc[:, r:r + 1] - csr[r:r + 1, :]
        dec = jnp.where(causal, jnp.exp(jnp.where(causal, seg, 0.0)), 0.0)
        return (cb * dec).astype(BF16)

    ys = []
    for k in range(npairs):
        x2 = xdt_b[:, LANES * k:LANES * (k + 1)]
        ya = jnp.dot(decay_mat(2 * k), x2, preferred_element_type=F32)
        yb = jnp.dot(decay_mat(2 * k + 1), x2, preferred_element_type=F32)
        ys.append(jnp.where(lo, ya, yb))
    y_diag = ys[0] if npairs == 1 else jnp.concatenate(ys, axis=1)

    state = state_ref[...]
    y_off = jnp.dot(cm, state.astype(BF16), preferred_element_type=F32) * jnp.exp(cs_e)
    y = y_diag + y_off + xs * dskip_ref[...]

    contrib = lax.dot_general(bm, (xdt * jnp.exp(last - cs_e)).astype(BF16),
                              (((0,), (0,)), ((), ())), preferred_element_type=F32)
    state_ref[...] = state * jnp.exp(last) + contrib

    yz = y * _silu(z_ref[...])
    ms = jnp.mean(yz * yz, axis=-1, keepdims=True)
    o_ref[...] = (yz * lax.rsqrt(ms + RMS_EPS) * na_ref[...]).astype(o_ref.dtype)


def _ssd(xbc, za, dtc, csc, csr, d_skip_e, norm_a, e_a):
    l = xbc.shape[0]
    gw = e_a // G_A
    r = gw // P_A
    assert gw % LANES == 0 and r % 2 == 0
    nb = e_a // N_A
    vmem = 4 * CHUNK * gw * 4 * 2 + 40 * CHUNK * gw * 4
    return pl.pallas_call(
        functools.partial(_ssd_kernel, npairs=gw // LANES),
        out_shape=jax.ShapeDtypeStruct((l, e_a), BF16),
        grid=(G_A, l // CHUNK),
        in_specs=[pl.BlockSpec((CHUNK, gw), lambda g, c: (c, g)),
                  pl.BlockSpec((CHUNK, N_A), lambda g, c: (c, nb + g)),
                  pl.BlockSpec((CHUNK, N_A), lambda g, c: (c, nb + G_A + g)),
                  pl.BlockSpec((CHUNK, gw), lambda g, c: (c, g)),
                  pl.BlockSpec((None, CHUNK, r), lambda g, c: (g, c, 0)),
                  pl.BlockSpec((None, CHUNK, r), lambda g, c: (g, c, 0)),
                  pl.BlockSpec((None, r, CHUNK), lambda g, c: (g, 0, c)),
                  pl.BlockSpec((1, gw), lambda g, c: (0, g)),
                  pl.BlockSpec((1, gw), lambda g, c: (0, g))],
        out_specs=pl.BlockSpec((CHUNK, gw), lambda g, c: (c, g)),
        scratch_shapes=[pltpu.VMEM((N_A, gw), F32)],
        compiler_params=_params(2, vmem),
        name="ssd_scan",
    )(xbc, xbc, xbc, za, dtc, csc, csr, d_skip_e, norm_a.reshape(1, e_a))


def _pool_kernel(u_ref, w_ref, sc_ref, o_ref, ext_ref, *, ts):
    gi = pl.program_id(0)
    s = pl.program_id(1)

    @pl.when(s == 0)
    def _():
        ext_ref[0:HALO, :] = jnp.zeros((HALO, ext_ref.shape[1]), F32)

    cur = u_ref[...]
    ext_ref[HALO:HALO + ts, :] = cur
    t = s * ts + lax.broadcasted_iota(jnp.int32, (ts, 1), 0)

    for k, win in enumerate(POOL_WINDOWS):
        @pl.when(gi == k)
        def _(win=win):
            acc = cur
            for j in range(1, win):
                acc = acc + ext_ref[HALO - j:HALO - j + ts, :]
            cnt = jnp.minimum(t + 1, win).astype(F32)
            pooled = acc / cnt - cur
            mixed = jnp.dot(pooled.astype(BF16), w_ref[...].astype(BF16),
                            preferred_element_type=F32)
            o_ref[...] = (mixed * sc_ref[...]).astype(o_ref.dtype)

    ext_ref[0:HALO, :] = cur[ts - HALO:ts, :]


def _pool(src, col_off, e_b, pool_w, pool_scale):
    l = src.shape[0]
    cp = e_b // len(POOL_WINDOWS)
    assert col_off % cp == 0 and cp % LANES == 0 and HALO >= max(POOL_WINDOWS) - 1
    ts = min(l, 1024)
    coff = col_off // cp
    vmem = 6 * ts * cp * 4 + 2 * cp * cp * 4 + 6 * ts * cp * 4
    return pl.pallas_call(
        functools.partial(_pool_kernel, ts=ts),
        out_shape=jax.ShapeDtypeStruct((l, e_b), BF16),
        grid=(len(POOL_WINDOWS), l // ts),
        in_specs=[pl.BlockSpec((ts, cp), lambda g, s: (s, coff + g)),
                  pl.BlockSpec((None, cp, cp), lambda g, s: (g, 0, 0)),
                  pl.BlockSpec((1, cp), lambda g, s: (0, g))],
        out_specs=pl.BlockSpec((ts, cp), lambda g, s: (s, g)),
        scratch_shapes=[pltpu.VMEM((HALO + ts, cp), F32)],
        compiler_params=_params(2, vmem),
        name="pool_mix",
    )(src, pool_w, pool_scale.reshape(1, e_b))


def _rope_kernel(pos_ref, inv_ref, cos_ref, sin_ref):
    ang = pos_ref[...].astype(F32) * inv_ref[...]
    lane = lax.broadcasted_iota(jnp.int32, ang.shape, 1)
    cos_ref[...] = jnp.cos(ang)
    sin_ref[...] = jnp.where(lane < DK_C // 2, -jnp.sin(ang), jnp.sin(ang))


def _rope_tables(positions):
    l = positions.shape[0]
    half = DK_C // 2
    inv = ROPE_BASE ** (-jnp.arange(half, dtype=F32) / half)
    inv = jnp.concatenate([inv, inv]).reshape(1, DK_C)
    ts = min(l, 1024)
    spec = pl.BlockSpec((ts, DK_C), lambda s: (s, 0))
    return pl.pallas_call(
        _rope_kernel,
        out_shape=(jax.ShapeDtypeStruct((l, DK_C), F32), jax.ShapeDtypeStruct((l, DK_C), F32)),
        grid=(l // ts,),
        in_specs=[pl.BlockSpec((ts, 1), lambda s: (s, 0)),
                  pl.BlockSpec((1, DK_C), lambda s: (0, 0))],
        out_specs=(spec, spec),
        compiler_params=_params(1, 0),
        name="rope_tables",
    )(positions.reshape(l, 1), inv)


def _ret_kernel(q_ref, k_ref, v_ref, g_ref, cos_ref, sin_ref, lg_ref, nc_ref, o_ref, state_ref,
                *, hb):
    c = pl.program_id(1)

    @pl.when(c == 0)
    def _():
        state_ref[...] = jnp.zeros(state_ref.shape, F32)

    cos = cos_ref[...]
    sin = sin_ref[...]
    rowi = lax.broadcasted_iota(jnp.int32, (CHUNK, CHUNK), 0)
    coli = lax.broadcasted_iota(jnp.int32, (CHUNK, CHUNK), 1)
    causal = rowi >= coli
    rel = jnp.where(causal, rowi - coli, 0).astype(F32)
    idx = rowi.astype(F32)
    scale = DK_C ** -0.5

    for j in range(hb):
        sl = slice(j * DK_C, (j + 1) * DK_C)
        lg = lg_ref[j]
        q = q_ref[:, sl]
        k = k_ref[:, sl]
        v = v_ref[:, sl].astype(BF16)
        qr = (q * cos + pltpu.roll(q, DK_C // 2, 1) * sin) * scale
        kr = k * cos + pltpu.roll(k, DK_C // 2, 1) * sin
        dmat = jnp.where(causal, jnp.exp(rel * lg), 0.0)
        scores = lax.dot_general(qr.astype(BF16), kr.astype(BF16), (((1,), (1,)), ((), ())),
                                 preferred_element_type=F32) * dmat
        y = jnp.dot(scores.astype(BF16), v, preferred_element_type=F32)
        st = state_ref[j]
        qd = qr * jnp.exp((idx + 1.0) * lg)
        y = y + jnp.dot(qd.astype(BF16), st.astype(BF16), preferred_element_type=F32)
        kd = kr * jnp.exp((CHUNK - 1.0 - idx) * lg)
        state_ref[j] = st * jnp.exp(CHUNK * lg) + lax.dot_general(
            kd.astype(BF16), v, (((0,), (0,)), ((), ())), preferred_element_type=F32)
        ret = y * lax.rsqrt(jnp.mean(y * y, axis=-1, keepdims=True) + RMS_EPS) * nc_ref[...]
        o_ref[:, sl] = (ret * _silu(g_ref[:, sl])).astype(o_ref.dtype)


def _retention(src, q_off, e_c, cos, sin, norm_c):
    l = src.shape[0]
    h_c = e_c // DV_C
    hb = min(h_c, 4)
    bw = hb * DK_C
    assert h_c % hb == 0 and q_off % bw == 0 and DK_C == DV_C == LANES
    lg = jnp.log1p(-jnp.exp2(-5.0 - jnp.arange(h_c, dtype=F32)))
    lg = jnp.broadcast_to(lg[:, None, None], (h_c, 1, LANES))
    nblk = e_c // bw
    qo = q_off // bw

    def col(which):
        return lambda h, c: (c, qo + which * nblk + h)

    tab = pl.BlockSpec((CHUNK, DK_C), lambda h, c: (c, 0))
    vmem = 12 * CHUNK * bw * 4 + 40 * CHUNK * CHUNK * 4
    return pl.pallas_call(
        functools.partial(_ret_kernel, hb=hb),
        out_shape=jax.ShapeDtypeStruct((l, e_c), BF16),
        grid=(nblk, l // CHUNK),
        in_specs=[pl.BlockSpec((CHUNK, bw), col(0)),
                  pl.BlockSpec((CHUNK, bw), col(1)),
                  pl.BlockSpec((CHUNK, bw), col(2)),
                  pl.BlockSpec((CHUNK, bw), col(3)),
                  tab, tab,
                  pl.BlockSpec((hb, 1, LANES), lambda h, c: (h, 0, 0)),
                  pl.BlockSpec((1, DV_C), lambda h, c: (0, 0))],
        out_specs=pl.BlockSpec((CHUNK, bw), lambda h, c: (c, h)),
        scratch_shapes=[pltpu.VMEM((hb, DK_C, DV_C), F32)],
        compiler_params=_params(2, vmem),
        name="retention",
    )(src, src, src, src, cos, sin, lg, norm_c.reshape(1, DV_C))


def _ffn_act_kernel(ug_ref, uv_ref, wg_ref, wv_ref, bg_ref, bv_ref, o_ref, eg_ref, ev_ref, *, ts):
    s = pl.program_id(1)

    @pl.when(s == 0)
    def _():
        eg_ref[0:HALO, :] = jnp.zeros((HALO, eg_ref.shape[1]), F32)
        ev_ref[0:HALO, :] = jnp.zeros((HALO, ev_ref.shape[1]), F32)

    def conv(u_ref, e_ref, w_ref, b_ref):
        cur = u_ref[...]
        e_ref[HALO:HALO + ts, :] = cur
        y = b_ref[...]
        for k in range(K_F):
            off = HALO - (K_F - 1) + k
            y = y + e_ref[off:off + ts, :] * w_ref[k:k + 1, :]
        e_ref[0:HALO, :] = cur[ts - HALO:ts, :]
        return y

    gate = conv(ug_ref, eg_ref, wg_ref, bg_ref)
    val = conv(uv_ref, ev_ref, wv_ref, bv_ref)
    o_ref[...] = (_silu(gate) * val).astype(o_ref.dtype)


def _ffn_act(up, conv_w, conv_b, d_ff):
    l = up.shape[0]
    ts = min(l, 1024)
    tc = _tile(d_ff, 512)
    nb = d_ff // tc
    vmem = 10 * ts * tc * 4 + 6 * ts * tc * 4
    cb = conv_b.reshape(1, 2 * d_ff)
    return pl.pallas_call(
        functools.partial(_ffn_act_kernel, ts=ts),
        out_shape=jax.ShapeDtypeStruct((l, d_ff), BF16),
        grid=(nb, l // ts),
        in_specs=[pl.BlockSpec((ts, tc), lambda j, s: (s, j)),
                  pl.BlockSpec((ts, tc), lambda j, s: (s, nb + j)),
                  pl.BlockSpec((K_F, tc), lambda j, s: (0, j)),
                  pl.BlockSpec((K_F, tc), lambda j, s: (0, nb + j)),
                  pl.BlockSpec((1, tc), lambda j, s: (0, j)),
                  pl.BlockSpec((1, tc), lambda j, s: (0, nb + j))],
        out_specs=pl.BlockSpec((ts, tc), lambda j, s: (s, j)),
        scratch_shapes=[pltpu.VMEM((HALO + ts, tc), F32), pltpu.VMEM((HALO + ts, tc), F32)],
        compiler_params=_params(2, vmem),
        name="ffn_act",
    )(up, up, conv_w, conv_w, cb, cb)


def kernel(x, positions, norm_mix, w_in, b_gate, conv_a_w, conv_a_b, dt_bias, a_log, d_skip, norm_a, pool_w, pool_scale, norm_c, w_br_a, w_br_b, w_br_c, w_out, norm_ffn, w_up, conv_f_w, conv_f_b, w_down, norm_f):
    b, l, d = x.shape
    assert b == 1 and l % CHUNK == 0
    depth = w_in.shape[0]
    e_a = w_br_a.shape[1]
    e_b = w_br_b.shape[1]
    e_c = w_br_c.shape[1]
    h_a = e_a // P_A
    conv_a = e_a + 2 * G_A * N_A
    d_ff = w_down.shape[1]
    r = h_a // G_A
    o_dt = e_a + conv_a
    o_rest = o_dt + h_a
    hp = -(-h_a // LANES) * LANES
    q_off = e_b
    gate_off = e_b + 4 * e_c

    xf = x.reshape(l, d)
    cos, sin = _rope_tables(positions.reshape(l))

    for i in range(depth):
        w_za = w_in[i][:, :o_dt].astype(BF16)
        w_dt = jnp.pad(w_in[i][:, o_dt:o_rest], ((0, 0), (0, hp - h_a))).astype(BF16)
        w_rest = w_in[i][:, o_rest:].astype(BF16)

        h = _rmsnorm(xf, norm_mix[i], BF16)
        za = _mm(h, w_za, tn=1024, name="proj_za")
        dt_raw = _mm(h, w_dt, name="proj_dt")
        rest = _mm(h, w_rest, tn=1024, name="proj_rest")

        xbc = _dwconv_silu(za, e_a, conv_a, conv_a_w[i], conv_a_b[i], F32, "ssd_conv")
        pad = (0, hp - h_a)
        dt, cs = _dt_prep(dt_raw, jnp.pad(dt_bias[i], pad).reshape(1, hp),
                          jnp.pad(a_log[i], pad).reshape(1, hp))
        dtc = dt[:, :h_a].reshape(l, G_A, r).transpose(1, 0, 2)
        csc = cs[:, :h_a].reshape(l, G_A, r).transpose(1, 0, 2)
        csr = cs[:, :h_a].reshape(l, G_A, r).transpose(1, 2, 0)
        d_skip_e = jnp.repeat(d_skip[i], P_A).reshape(1, e_a)
        y_a = _ssd(xbc, za, dtc, csc, csr, d_skip_e, norm_a[i], e_a)

        y_b = _pool(rest, 0, e_b, pool_w[i], pool_scale[i])

        y_c = _retention(rest, q_off, e_c, cos, sin, norm_c[i])

        bg = b_gate[i]
        m = _mm(y_a, w_br_a[i].astype(BF16), mode="gate", gate_src=rest, gate_off=gate_off,
                gate_bias=bg[:d], name="merge_a")
        m = _mm(y_b, w_br_b[i].astype(BF16), mode="gate", gate_src=rest, gate_off=gate_off + d,
                gate_bias=bg[d:2 * d], prev=m, name="merge_b")
        m = _mm(y_c, w_br_c[i].astype(BF16), mode="gate", gate_src=rest,
                gate_off=gate_off + 2 * d, gate_bias=bg[2 * d:], prev=m, out_dtype=BF16,
                name="merge_c")
        xf = _mm(m, w_out[i].astype(BF16), mode="resid", resid=xf, name="out_proj")

        h = _rmsnorm(xf, norm_ffn[i], BF16)
        up = _mm(h, w_up[i].astype(BF16), name="ffn_up")
        act = _ffn_act(up, conv_f_w[i], conv_f_b[i], d_ff)
        tk = _tile(d_ff, 5632)
        xf = _mm(act, w_down[i].astype(BF16), mode="resid", resid=xf, tk=tk, name="ffn_down")

    return _rmsnorm(xf, norm_f, x.dtype).reshape(b, l, d)
```
